```python
import math
import jax
import jax.numpy as jnp
from jax import lax
import numpy as np


D_MODEL = 2048
BATCH = 2
SEQ = 16384
DEPTH = 4
DEC_BATCH = 4
DEC_SEQ = 8192
PAST_LEN = 128

GRID_W = 64
Q_BLOCK = 128
ROPE_THETA = 10000.0
NORM_EPS = 1e-6

MLA_WIDTH = D_MODEL // 4
GQA_WIDTH = D_MODEL // 2
SSM_WIDTH = D_MODEL - MLA_WIDTH - GQA_WIDTH
MIX_WIDTH = MLA_WIDTH + GQA_WIDTH + SSM_WIDTH

MLA_NOPE = 128
MLA_ROPE = 64
MLA_V = 128
MLA_HEADS = MLA_WIDTH // MLA_V
MLA_Q_LORA = D_MODEL // 4
MLA_KV_LORA = D_MODEL // 8

GQA_HEAD_DIM = 128
GQA_HEADS = GQA_WIDTH // GQA_HEAD_DIM
GQA_KV_HEADS = GQA_HEADS // 4

SSM_GROUP = 16
SSM_GROUPS = SSM_WIDTH // SSM_GROUP
SSM_STATE = 64
LOG_DT_MIN = math.log(1e-3)
LOG_DT_MAX = math.log(1e-1)

D_FF = 4 * D_MODEL

MLA_IN = MLA_Q_LORA + MLA_KV_LORA + MLA_ROPE
GQA_IN = (GQA_HEADS + 2 * GQA_KV_HEADS) * GQA_HEAD_DIM
IN_COLS = MLA_IN + GQA_IN + SSM_WIDTH

kernel_name = 'hybrid_mla_gqa_s5_encoder'


def rms_norm(x, g):
    xf = x.astype(jnp.float32)
    y = xf * lax.rsqrt(jnp.mean(xf * xf, axis=-1, keepdims=True) + NORM_EPS)
    return (y * g.astype(jnp.float32)).astype(x.dtype)


def axial_rope_tables(n_tokens, rot_dim):
    n_rows = n_tokens // GRID_W
    row = jnp.repeat(jnp.arange(n_rows, dtype=jnp.float32), GRID_W)
    col = jnp.tile(jnp.arange(GRID_W, dtype=jnp.float32), n_rows)
    axis_dim = rot_dim // 2
    inv_freq = ROPE_THETA ** (-jnp.arange(0, axis_dim, 2, dtype=jnp.float32) / axis_dim)
    ang = jnp.concatenate([row[:, None] * inv_freq, col[:, None] * inv_freq], axis=-1)
    return jnp.cos(ang), jnp.sin(ang)


def apply_rope(x, cos, sin):
    half = x.shape[-1] // 2
    xf = x.astype(jnp.float32)
    x1, x2 = xf[..., :half], xf[..., half:]
    c = cos[None, :, None, :]
    s = sin[None, :, None, :]
    return jnp.concatenate([x1 * c - x2 * s, x2 * c + x1 * s], axis=-1).astype(x.dtype)


def blocked_attention(q, k, v, scale):
    b, l, hk, g, dk = q.shape
    nb = l // Q_BLOCK
    qb = jnp.moveaxis(q.reshape(b, nb, Q_BLOCK, hk, g, dk), 1, 0)

    def attend(q_blk):
        s = jnp.einsum('bqhgd,bkhd->bhgqk', q_blk, k, preferred_element_type=jnp.float32) * scale
        p = jax.nn.softmax(s, axis=-1)
        return jnp.einsum('bhgqk,bkhd->bqhgd', p.astype(v.dtype), v)

    o = lax.map(attend, qb)
    return jnp.moveaxis(o, 0, 1).reshape(b, l, hk * g * v.shape[-1])


def mla_mixer(c_q, c_kv, k_rope, q_norm, w_uq, kv_norm, w_ukv, cos, sin):
    b, l, _ = c_q.shape
    q = (rms_norm(c_q, q_norm) @ w_uq).reshape(b, l, MLA_HEADS, MLA_NOPE + MLA_ROPE)
    q_nope = q[..., :MLA_NOPE]
    q_rot = apply_rope(q[..., MLA_NOPE:], cos, sin)
    kv = (rms_norm(c_kv, kv_norm) @ w_ukv).reshape(b, l, MLA_HEADS, MLA_NOPE + MLA_V)
    k_nope, v = kv[..., :MLA_NOPE], kv[..., MLA_NOPE:]
    k_rot = apply_rope(k_rope[:, :, None, :], cos, sin)
    q_full = jnp.concatenate([q_nope, q_rot], axis=-1)[:, :, :, None, :]
    k_full = jnp.concatenate([k_nope, jnp.broadcast_to(k_rot, (b, l, MLA_HEADS, MLA_ROPE))], axis=-1)
    return blocked_attention(q_full, k_full, v, (MLA_NOPE + MLA_ROPE) ** -0.5)


def gqa_mixer(q, k, v, q_norm, k_norm, cos, sin):
    b, l, _ = q.shape
    q = apply_rope(rms_norm(q.reshape(b, l, GQA_HEADS, GQA_HEAD_DIM), q_norm), cos, sin)
    k = apply_rope(rms_norm(k.reshape(b, l, GQA_KV_HEADS, GQA_HEAD_DIM), k_norm), cos, sin)
    v = v.reshape(b, l, GQA_KV_HEADS, GQA_HEAD_DIM)
    q = q.reshape(b, l, GQA_KV_HEADS, GQA_HEADS // GQA_KV_HEADS, GQA_HEAD_DIM)
    return blocked_attention(q, k, v, GQA_HEAD_DIM ** -0.5)


def zoh_discretise(a_re, a_im, log_dt, b_re, b_im):
    lam = lax.complex(a_re.astype(jnp.float32), a_im.astype(jnp.float32))
    dt = jnp.exp(log_dt.astype(jnp.float32))[:, None]
    lam_bar = jnp.exp(lam * dt)
    b = lax.complex(b_re.astype(jnp.float32), b_im.astype(jnp.float32))
    b_bar = ((lam_bar - 1.0) / lam)[..., None] * b
    return jnp.real(lam_bar), jnp.imag(lam_bar), jnp.real(b_bar), jnp.imag(b_bar)


def complex_affine_combine(e1, e2):
    a1r, a1i, b1r, b1i = e1
    a2r, a2i, b2r, b2i = e2
    return (a2r * a1r - a2i * a1i,
            a2r * a1i + a2i * a1r,
            a2r * b1r - a2i * b1i + b2r,
            a2r * b1i + a2i * b1r + b2i)


def ssm_direction(u, a_re, a_im, log_dt, b_re, b_im, c_re, c_im, reverse):
    lb_re, lb_im, bb_re, bb_im = zoh_discretise(a_re, a_im, log_dt, b_re, b_im)
    bu_re = jnp.einsum('blgh,gph->lbgp', u, bb_re)
    bu_im = jnp.einsum('blgh,gph->lbgp', u, bb_im)
    lam_re = jnp.broadcast_to(lb_re[None, None], bu_re.shape)
    lam_im = jnp.broadcast_to(lb_im[None, None], bu_re.shape)
    _, _, s_re, s_im = lax.associative_scan(
        complex_affine_combine, (lam_re, lam_im, bu_re, bu_im), reverse=reverse, axis=0)
    return (jnp.einsum('lbgp,ghp->blgh', s_re, c_re.astype(jnp.float32))
            - jnp.einsum('lbgp,ghp->blgh', s_im, c_im.astype(jnp.float32)))


def ssm_mixer(u, a_re, a_im, log_dt, b_re, b_im, c_re, c_im, d, w_glu):
    b, l, _ = u.shape
    uf = u.astype(jnp.float32).reshape(b, l, SSM_GROUPS, SSM_GROUP)
    y = d.astype(jnp.float32) * uf
    for direction, rev in ((0, False), (1, True)):
        y = y + ssm_direction(uf, a_re[direction], a_im[direction], log_dt[direction],
                              b_re[direction], b_im[direction], c_re[direction], c_im[direction], rev)
    y = jax.nn.gelu(y.reshape(b, l, SSM_WIDTH)).astype(u.dtype)
    return y * jax.nn.sigmoid(y @ w_glu)


def encoder_trunk(x, pre_mix_norm, w_in, mla_q_norm, mla_w_uq, mla_kv_norm, mla_w_ukv,
                  gqa_q_norm, gqa_k_norm, ssm_a_re, ssm_a_im, ssm_log_dt, ssm_b_re, ssm_b_im,
                  ssm_c_re, ssm_c_im, ssm_d, ssm_w_glu, w_out, post_mix_norm, pre_mlp_norm,
                  w_up, w_down, post_mlp_norm):
    n_tokens = x.shape[1]
    cos_m, sin_m = axial_rope_tables(n_tokens, MLA_ROPE)
    cos_g, sin_g = axial_rope_tables(n_tokens, GQA_HEAD_DIM)
    o1 = MLA_Q_LORA
    o2 = o1 + MLA_KV_LORA
    o3 = MLA_IN
    o4 = o3 + GQA_HEADS * GQA_HEAD_DIM
    o5 = o4 + GQA_KV_HEADS * GQA_HEAD_DIM
    o6 = o5 + GQA_KV_HEADS * GQA_HEAD_DIM
    for i in range(DEPTH):
        h = rms_norm(x, pre_mix_norm[i])
        proj = h @ w_in[i]
        y_mla = mla_mixer(proj[..., :o1], proj[..., o1:o2], proj[..., o2:o3],
                          mla_q_norm[i], mla_w_uq[i], mla_kv_norm[i], mla_w_ukv[i], cos_m, sin_m)
        y_gqa = gqa_mixer(proj[..., o3:o4], proj[..., o4:o5], proj[..., o5:o6],
                          gqa_q_norm[i], gqa_k_norm[i], cos_g, sin_g)
        y_ssm = ssm_mixer(proj[..., o6:], ssm_a_re[i], ssm_a_im[i], ssm_log_dt[i], ssm_b_re[i],
                          ssm_b_im[i], ssm_c_re[i], ssm_c_im[i], ssm_d[i], ssm_w_glu[i])
        mix = jnp.concatenate([y_mla, y_gqa, y_ssm], axis=-1) @ w_out[i]
        x = x + rms_norm(mix, post_mix_norm[i])
        h = rms_norm(x, pre_mlp_norm[i])
        m = jnp.square(jax.nn.relu(h @ w_up[i])) @ w_down[i]
        x = x + rms_norm(m, post_mlp_norm[i])
    return x


def setup_inputs(seed: int = 0) -> dict:
    key = jax.random.key(seed)
    ks = jax.random.split(key, 25)
    f32 = jnp.float32

    def dense(k, shape, fan_in):
        return jax.random.normal(k, shape, f32) * fan_in ** -0.5

    def gain(k, shape):
        return 1.0 + 0.02 * jax.random.normal(k, shape, f32)

    n_idx = jnp.arange(SSM_STATE, dtype=f32)
    a_shape = (DEPTH, 2, SSM_GROUPS, SSM_STATE)
    b_shape = (DEPTH, 2, SSM_GROUPS, SSM_STATE, SSM_GROUP)
    c_shape = (DEPTH, 2, SSM_GROUPS, SSM_GROUP, SSM_STATE)
    return {
        'x_prompt': jax.random.normal(ks[0], (BATCH, SEQ, D_MODEL), f32),
        'x_sample': jax.random.normal(ks[1], (DEC_BATCH, DEC_SEQ, D_MODEL), f32),
        'pre_mix_norm': gain(ks[2], (DEPTH, D_MODEL)),
        'w_in': dense(ks[3], (DEPTH, D_MODEL, IN_COLS), D_MODEL),
        'mla_q_norm': gain(ks[4], (DEPTH, MLA_Q_LORA)),
        'mla_w_uq': dense(ks[5], (DEPTH, MLA_Q_LORA, MLA_HEADS * (MLA_NOPE + MLA_ROPE)), MLA_Q_LORA),
        'mla_kv_norm': gain(ks[6], (DEPTH, MLA_KV_LORA)),
        'mla_w_ukv': dense(ks[7], (DEPTH, MLA_KV_LORA, MLA_HEADS * (MLA_NOPE + MLA_V)), MLA_KV_LORA),
        'gqa_q_norm': gain(ks[8], (DEPTH, GQA_HEAD_DIM)),
        'gqa_k_norm': gain(ks[9], (DEPTH, GQA_HEAD_DIM)),
        'ssm_a_re': -0.5 + 0.01 * jax.random.normal(ks[10], a_shape, f32),
        'ssm_a_im': math.pi * n_idx + 0.01 * jax.random.normal(ks[11], a_shape, f32),
        'ssm_log_dt': jax.random.uniform(ks[12], (DEPTH, 2, SSM_GROUPS), f32, LOG_DT_MIN, LOG_DT_MAX),
        'ssm_b_re': jax.random.normal(ks[13], b_shape, f32) * (2 * SSM_GROUP) ** -0.5,
        'ssm_b_im': jax.random.normal(ks[14], b_shape, f32) * (2 * SSM_GROUP) ** -0.5,
        'ssm_c_re': jax.random.normal(ks[15], c_shape, f32) * SSM_STATE ** -0.5,
        'ssm_c_im': jax.random.normal(ks[16], c_shape, f32) * SSM_STATE ** -0.5,
        'ssm_d': jax.random.normal(ks[17], (DEPTH, SSM_GROUPS, SSM_GROUP), f32),
        'ssm_w_glu': dense(ks[18], (DEPTH, SSM_WIDTH, SSM_WIDTH), SSM_WIDTH),
        'w_out': dense(ks[19], (DEPTH, MIX_WIDTH, D_MODEL), MIX_WIDTH),
        'post_mix_norm': gain(ks[20], (DEPTH, D_MODEL)),
        'pre_mlp_norm': gain(ks[21], (DEPTH, D_MODEL)),
        'w_up': dense(ks[22], (DEPTH, D_MODEL, D_FF), D_MODEL),
        'w_down': dense(ks[23], (DEPTH, D_FF, D_MODEL), D_FF),
        'post_mlp_norm': gain(ks[24], (DEPTH, D_MODEL)),
    }


def reference(x_prompt, x_sample, pre_mix_norm, w_in, mla_q_norm, mla_w_uq, mla_kv_norm, mla_w_ukv,
              gqa_q_norm, gqa_k_norm, ssm_a_re, ssm_a_im, ssm_log_dt, ssm_b_re, ssm_b_im,
              ssm_c_re, ssm_c_im, ssm_d, ssm_w_glu, w_out, post_mix_norm, pre_mlp_norm,
              w_up, w_down, post_mlp_norm):
    y_prompt = encoder_trunk(x_prompt, pre_mix_norm, w_in, mla_q_norm, mla_w_uq, mla_kv_norm,
                             mla_w_ukv, gqa_q_norm, gqa_k_norm, ssm_a_re, ssm_a_im, ssm_log_dt,
                             ssm_b_re, ssm_b_im, ssm_c_re, ssm_c_im, ssm_d, ssm_w_glu, w_out,
                             post_mix_norm, pre_mlp_norm, w_up, w_down, post_mlp_norm)
    y_sample = encoder_trunk(x_sample, pre_mix_norm, w_in, mla_q_norm, mla_w_uq, mla_kv_norm,
                             mla_w_ukv, gqa_q_norm, gqa_k_norm, ssm_a_re, ssm_a_im, ssm_log_dt,
                             ssm_b_re, ssm_b_im, ssm_c_re, ssm_c_im, ssm_d, ssm_w_glu, w_out,
                             post_mix_norm, pre_mlp_norm, w_up, w_down, post_mlp_norm)
    return (y_prompt, y_sample)
```

```python
import functools
import math

import jax
import jax.numpy as jnp
from jax import lax
from jax.experimental import pallas as pl
from jax.experimental.pallas import tpu as pltpu

F32 = jnp.float32
BF16 = jnp.bfloat16

D_MODEL = 2048
DEPTH = 4
GRID_W = 64
ROPE_THETA = 10000.0
NORM_EPS = 1e-6

MLA_WIDTH = D_MODEL // 4
GQA_WIDTH = D_MODEL // 2
SSM_WIDTH = D_MODEL - MLA_WIDTH - GQA_WIDTH
MLA_NOPE = 128
MLA_ROPE = 64
MLA_V = 128
MLA_HEADS = MLA_WIDTH // MLA_V
MLA_Q_LORA = D_MODEL // 4
MLA_KV_LORA = D_MODEL // 8
MLA_QK_PAD = 256
GQA_HEAD_DIM = 128
GQA_HEADS = GQA_WIDTH // GQA_HEAD_DIM
GQA_KV_HEADS = GQA_HEADS // 4
GQA_GROUP = GQA_HEADS // GQA_KV_HEADS
SSM_GROUP = 16
SSM_GROUPS = SSM_WIDTH // SSM_GROUP
SSM_STATE = 64
SSM_STATES = SSM_GROUPS * SSM_STATE
SSM_HALF_IN = SSM_WIDTH // 2
SSM_HALF_STATES = SSM_STATES // 2
D_FF = 4 * D_MODEL

P_CQ = 0
P_CKV = P_CQ + MLA_Q_LORA
P_KR = P_CKV + MLA_KV_LORA
P_KRS = P_KR + 128
P_GQ = P_KRS + 128
P_GK = P_GQ + GQA_WIDTH
P_GV = P_GK + GQA_KV_HEADS * GQA_HEAD_DIM
P_U = P_GV + GQA_KV_HEADS * GQA_HEAD_DIM
P_COLS = P_U + SSM_WIDTH

T_MC = 0
T_MS = 256
T_KC = 512
T_KS = 640
T_GC = 768
T_GS = 896
T_COLS = 1024

LOG2E = math.log2(math.e)
MLA_QSCALE = (MLA_NOPE + MLA_ROPE) ** -0.5 * LOG2E
GQA_QSCALE = GQA_HEAD_DIM ** -0.5 * LOG2E

VMEM_LIMIT = 56 * 1024 * 1024

TM_PROJ = 1024
TN_PROJ = 1024
TP_PREP = 512
TM_OUT = 512
TM_MLP = 512
TF_MLP = 1024
TK_ATTN = 512
NQ_ATTN = 1024
TT_SSM = 512


def _rms(x, g):
    ms = jnp.mean(x * x, axis=-1, keepdims=True)
    return x * lax.rsqrt(ms + NORM_EPS) * g


def _params(sem):
    return pltpu.CompilerParams(dimension_semantics=sem, vmem_limit_bytes=VMEM_LIMIT)


def _in_proj_kernel(x_ref, g_ref, w_ref, o_ref, h_ref):
    @pl.when(pl.program_id(1) == 0)
    def _():
        h_ref[...] = _rms(x_ref[...], g_ref[...]).astype(BF16)

    o_ref[...] = jnp.dot(h_ref[...], w_ref[...], preferred_element_type=F32)


def _in_proj(x, g, w):
    n = x.shape[0]
    return pl.pallas_call(
        _in_proj_kernel,
        grid=(n // TM_PROJ, P_COLS // TN_PROJ),
        in_specs=[
            pl.BlockSpec((TM_PROJ, D_MODEL), lambda i, j: (i, 0)),
            pl.BlockSpec((1, D_MODEL), lambda i, j: (0, 0)),
            pl.BlockSpec((D_MODEL, TN_PROJ), lambda i, j: (0, j)),
        ],
        out_specs=pl.BlockSpec((TM_PROJ, TN_PROJ), lambda i, j: (i, j)),
        out_shape=jax.ShapeDtypeStruct((n, P_COLS), F32),
        scratch_shapes=[pltpu.VMEM((TM_PROJ, D_MODEL), BF16)],
        compiler_params=_params(("arbitrary", "arbitrary")),
        name="in_proj",
    )(x, g, w)


def _prep_kernel(p_ref, t_ref, qn_ref, wa_ref, wb_ref, kvn_ref, wkv_ref, gqn_ref, gkn_ref,
                 qm_ref, km_ref, vm_ref, qg_ref, kg_ref, vg_ref):
    nq = _rms(p_ref[:, P_CQ:P_CQ + MLA_Q_LORA], qn_ref[...]).astype(BF16)
    a = jnp.dot(nq, wa_ref[...], preferred_element_type=F32)
    b = jnp.dot(nq, wb_ref[...], preferred_element_type=F32)
    cos_m = t_ref[:, T_MC:T_MC + MLA_QK_PAD]
    sin_m = t_ref[:, T_MS:T_MS + MLA_QK_PAD]
    for h in range(MLA_HEADS):
        sl = slice(h * MLA_QK_PAD, (h + 1) * MLA_QK_PAD)
        qm_ref[:, sl] = ((a[:, sl] * cos_m + b[:, sl] * sin_m) * MLA_QSCALE).astype(BF16)

    nkv = _rms(p_ref[:, P_CKV:P_CKV + MLA_KV_LORA], kvn_ref[...]).astype(BF16)
    kv = jnp.dot(nkv, wkv_ref[...], preferred_element_type=F32)
    k_rot = (p_ref[:, P_KR:P_KR + 128] * t_ref[:, T_KC:T_KC + 128]
             + p_ref[:, P_KRS:P_KRS + 128] * t_ref[:, T_KS:T_KS + 128]).astype(BF16)
    for h in range(MLA_HEADS):
        base = h * (MLA_NOPE + MLA_V)
        km_ref[:, h * MLA_QK_PAD:h * MLA_QK_PAD + MLA_NOPE] = kv[:, base:base + MLA_NOPE].astype(BF16)
        km_ref[:, h * MLA_QK_PAD + MLA_NOPE:(h + 1) * MLA_QK_PAD] = k_rot
        vm_ref[:, h * MLA_V:(h + 1) * MLA_V] = kv[:, base + MLA_NOPE:base + MLA_NOPE + MLA_V].astype(BF16)

    cos_g = t_ref[:, T_GC:T_GC + 128]
    sin_g = t_ref[:, T_GS:T_GS + 128]

    def rope_g(x, g):
        y = _rms(x, g)
        return y * cos_g + pltpu.roll(y, GQA_HEAD_DIM // 2, 1) * sin_g

    for h in range(GQA_HEADS):
        sl = slice(h * GQA_HEAD_DIM, (h + 1) * GQA_HEAD_DIM)
        x = p_ref[:, P_GQ + h * GQA_HEAD_DIM:P_GQ + (h + 1) * GQA_HEAD_DIM]
        qg_ref[:, sl] = (rope_g(x, gqn_ref[...]) * GQA_QSCALE).astype(BF16)
    for h in range(GQA_KV_HEADS):
        sl = slice(h * GQA_HEAD_DIM, (h + 1) * GQA_HEAD_DIM)
        x = p_ref[:, P_GK + h * GQA_HEAD_DIM:P_GK + (h + 1) * GQA_HEAD_DIM]
        kg_ref[:, sl] = rope_g(x, gkn_ref[...]).astype(BF16)
    vg_ref[...] = p_ref[:, P_GV:P_GV + GQA_KV_HEADS * GQA_HEAD_DIM].astype(BF16)


def _prep(proj, table, seq_len, lw):
    n = proj.shape[0]
    t_blocks = seq_len // TP_PREP
    full = lambda r, c: pl.BlockSpec((r, c), lambda i: (0, 0))
    row = lambda c: pl.BlockSpec((TP_PREP, c), lambda i: (i, 0))
    kvw = GQA_KV_HEADS * GQA_HEAD_DIM
    return pl.pallas_call(
        _prep_kernel,
        grid=(n // TP_PREP,),
        in_specs=[
            row(P_COLS),
            pl.BlockSpec((TP_PREP, T_COLS), lambda i: (i % t_blocks, 0)),
            full(1, MLA_Q_LORA), full(MLA_Q_LORA, MLA_HEADS * MLA_QK_PAD),
            full(MLA_Q_LORA, MLA_HEADS * MLA_QK_PAD),
            full(1, MLA_KV_LORA), full(MLA_KV_LORA, MLA_HEADS * (MLA_NOPE + MLA_V)),
            full(1, GQA_HEAD_DIM), full(1, GQA_HEAD_DIM),
        ],
        out_specs=[row(MLA_HEADS * MLA_QK_PAD), row(MLA_HEADS * MLA_QK_PAD), row(MLA_WIDTH),
                   row(GQA_WIDTH), row(kvw), row(kvw)],
        out_shape=[
            jax.ShapeDtypeStruct((n, MLA_HEADS * MLA_QK_PAD), BF16),
            jax.ShapeDtypeStruct((n, MLA_HEADS * MLA_QK_PAD), BF16),
            jax.ShapeDtypeStruct((n, MLA_WIDTH), BF16),
            jax.ShapeDtypeStruct((n, GQA_WIDTH), BF16),
            jax.ShapeDtypeStruct((n, kvw), BF16),
            jax.ShapeDtypeStruct((n, kvw), BF16),
        ],
        compiler_params=_params(("arbitrary",)),
        name="prep",
    )(proj, table, lw["mla_q_norm"], lw["w_uq_a"], lw["w_uq_b"], lw["mla_kv_norm"], lw["w_ukv"],
      lw["gqa_q_norm"], lw["gqa_k_norm"])


def _attn_kernel(q_ref, k_ref, vt_ref, o_ref, m_ref, l_ref, acc_ref, *, group, tq, tk, nk, dk, dv):
    if group == 1:
        q = q_ref[0]
    else:
        q = jnp.concatenate([q_ref[0, :, g * dk:(g + 1) * dk] for g in range(group)], axis=0)
    m_ref[...] = jnp.full(m_ref.shape, -jnp.inf, F32)
    l_ref[...] = jnp.zeros(l_ref.shape, F32)
    acc_ref[...] = jnp.zeros(acc_ref.shape, F32)

    def body(ki, carry):
        k = k_ref[0, pl.ds(pl.multiple_of(ki * tk, tk), tk), :]
        s = lax.dot_general(k, q, (((1,), (1,)), ((), ())), preferred_element_type=F32)
        m_prev = m_ref[...]
        m_new = jnp.maximum(m_prev, jnp.max(s, axis=0, keepdims=True))
        alpha = jnp.exp2(m_prev - m_new)
        p = jnp.exp2(s - m_new)
        l_ref[...] = alpha * l_ref[...] + jnp.sum(p, axis=0, keepdims=True)
        pv = jnp.dot(vt_ref[0, 0, ki], p.astype(BF16), preferred_element_type=F32)
        acc_ref[...] = alpha * acc_ref[...] + pv
        m_ref[...] = m_new
        return carry

    lax.fori_loop(0, nk, body, 0)
    o = (acc_ref[...] / l_ref[...]).T
    for g in range(group):
        o_ref[0, :, g * dv:(g + 1) * dv] = o[g * tq:(g + 1) * tq].astype(BF16)


def _attention(q, k, v, *, kv_heads, group, dk, dv):
    b, l, _ = q.shape
    tq = NQ_ATTN // group
    tk = TK_ATTN
    nk = l // tk
    vt = v.reshape(b, nk, tk, kv_heads, dv).transpose(0, 3, 1, 4, 2)
    kern = functools.partial(_attn_kernel, group=group, tq=tq, tk=tk, nk=nk, dk=dk, dv=dv)
    return pl.pallas_call(
        kern,
        grid=(b, kv_heads, l // tq),
        in_specs=[
            pl.BlockSpec((1, tq, group * dk), lambda bi, gi, qi: (bi, qi, gi)),
            pl.BlockSpec((1, l, dk), lambda bi, gi, qi: (bi, 0, gi)),
            pl.BlockSpec((1, 1, nk, dv, tk), lambda bi, gi, qi: (bi, gi, 0, 0, 0)),
        ],
        out_specs=pl.BlockSpec((1, tq, group * dv), lambda bi, gi, qi: (bi, qi, gi)),
        out_shape=jax.ShapeDtypeStruct((b, l, kv_heads * group * dv), BF16),
        scratch_shapes=[
            pltpu.VMEM((1, NQ_ATTN), F32),
            pltpu.VMEM((1, NQ_ATTN), F32),
            pltpu.VMEM((dv, NQ_ATTN), F32),
        ],
        compiler_params=_params(("arbitrary", "arbitrary", "arbitrary")),
        name="attention",
    )(q, k, vt)


def _ssm_kernel(*refs, tt, reverse):
    if reverse:
        u_ref, wb_ref, wc_ref, tab_ref, yf_ref, d_ref, wglu_ref, o_ref, s_ref, carry_ref = refs
    else:
        u_ref, wb_ref, wc_ref, tab_ref, o_ref, s_ref, carry_ref = refs
    nblk = tt // 8
    hs = SSM_HALF_STATES

    @pl.when(pl.program_id(1) == 0)
    def _():
        carry_ref[...] = jnp.zeros(carry_ref.shape, F32)

    u = u_ref[0]
    ub = u.astype(BF16)
    ys = []
    for kb in range(2):
        s_ref[...] = jnp.dot(ub[:, kb * SSM_HALF_IN:(kb + 1) * SSM_HALF_IN], wb_ref[kb],
                             preferred_element_type=F32)

        def blk(i, carry, kb=kb):
            cr, ci = carry
            r = (nblk - 1 - i) if reverse else i
            row = pl.multiple_of(r * 8, 8)
            xr = s_ref[pl.ds(row, 8), :hs]
            xi = s_ref[pl.ds(row, 8), hs:]
            for j, d in enumerate((1, 2, 4)):
                sh = (8 - d) if reverse else d
                sr = pltpu.roll(xr, sh, 0)
                si = pltpu.roll(xi, sh, 0)
                lr = tab_ref[kb, j, 0]
                li = tab_ref[kb, j, 1]
                xr, xi = xr + lr * sr - li * si, xi + lr * si + li * sr
            lr = tab_ref[kb, 3, 0]
            li = tab_ref[kb, 3, 1]
            xr, xi = xr + lr * cr - li * ci, xi + lr * ci + li * cr
            s_ref[pl.ds(row, 8), :hs] = xr
            s_ref[pl.ds(row, 8), hs:] = xi
            last = 0 if reverse else 7
            return xr[last:last + 1], xi[last:last + 1]

        cr, ci = lax.fori_loop(0, nblk, blk, (carry_ref[kb, 0], carry_ref[kb, 1]))
        carry_ref[kb, 0] = cr
        carry_ref[kb, 1] = ci
        ys.append(jnp.dot(s_ref[...].astype(BF16), wc_ref[kb], preferred_element_type=F32))
    y = jnp.concatenate(ys, axis=1)
    if reverse:
        y = jax.nn.gelu(d_ref[...] * u + yf_ref[0] + y)
        gate = jnp.dot(y.astype(BF16), wglu_ref[...], preferred_element_type=F32)
        o_ref[0] = (y * jax.nn.sigmoid(gate)).astype(BF16)
    else:
        o_ref[0] = y


def _ssm_direction(proj3, lw, direction, y_fwd=None):
    b, l, _ = proj3.shape
    tt = TT_SSM
    nt = l // tt
    reverse = direction == 1
    tmap = (lambda bi, ti: (bi, nt - 1 - ti, 0)) if reverse else (lambda bi, ti: (bi, ti, 0))
    umap = (lambda bi, ti: (bi, nt - 1 - ti, P_U // SSM_WIDTH)) if reverse else (
        lambda bi, ti: (bi, ti, P_U // SSM_WIDTH))
    const = lambda shape: pl.BlockSpec(shape, lambda bi, ti: (0,) * len(shape))
    in_specs = [
        pl.BlockSpec((1, tt, SSM_WIDTH), umap),
        const((2, SSM_HALF_IN, SSM_STATES)),
        const((2, SSM_STATES, SSM_HALF_IN)),
        const((2, 4, 2, 8, SSM_HALF_STATES)),
    ]
    args = [proj3, lw["ssm_wb"][direction], lw["ssm_wc"][direction], lw["ssm_tab"][direction]]
    if reverse:
        in_specs += [pl.BlockSpec((1, tt, SSM_WIDTH), tmap), const((1, SSM_WIDTH)),
                     const((SSM_WIDTH, SSM_WIDTH))]
        args += [y_fwd, lw["ssm_d"], lw["w_glu"]]
    return pl.pallas_call(
        functools.partial(_ssm_kernel, tt=tt, reverse=reverse),
        grid=(b, nt),
        in_specs=in_specs,
        out_specs=pl.BlockSpec((1, tt, SSM_WIDTH), tmap),
        out_shape=jax.ShapeDtypeStruct((b, l, SSM_WIDTH), BF16 if reverse else F32),
        scratch_shapes=[
            pltpu.VMEM((tt, SSM_STATES), F32),
            pltpu.VMEM((2, 2, 1, SSM_HALF_STATES), F32),
        ],
        compiler_params=_params(("arbitrary", "arbitrary")),
        name="ssm_bwd" if reverse else "ssm_fwd",
    )(*args)


def _out_proj_kernel(x_ref, ym_ref, yg_ref, ys_ref, w_ref, g_ref, o_ref):
    o1 = MLA_WIDTH
    o2 = o1 + GQA_WIDTH
    mix = jnp.dot(ym_ref[...], w_ref[:o1, :], preferred_element_type=F32)
    mix += jnp.dot(yg_ref[...], w_ref[o1:o2, :], preferred_element_type=F32)
    mix += jnp.dot(ys_ref[...], w_ref[o2:, :], preferred_element_type=F32)
    o_ref[...] = x_ref[...] + _rms(mix, g_ref[...])


def _out_proj(x, ym, yg, ys, w, g):
    n = x.shape[0]
    row = lambda c: pl.BlockSpec((TM_OUT, c), lambda i: (i, 0))
    return pl.pallas_call(
        _out_proj_kernel,
        grid=(n // TM_OUT,),
        in_specs=[row(D_MODEL), row(MLA_WIDTH), row(GQA_WIDTH), row(SSM_WIDTH),
                  pl.BlockSpec((D_MODEL, D_MODEL), lambda i: (0, 0)),
                  pl.BlockSpec((1, D_MODEL), lambda i: (0, 0))],
        out_specs=row(D_MODEL),
        out_shape=jax.ShapeDtypeStruct((n, D_MODEL), F32),
        compiler_params=_params(("arbitrary",)),
        name="out_proj",
    )(x, ym, yg, ys, w, g)


def _mlp_kernel(x_ref, g1_ref, wu_ref, wd_ref, g2_ref, o_ref, h_ref, acc_ref):
    j = pl.program_id(1)

    @pl.when(j == 0)
    def _():
        h_ref[...] = _rms(x_ref[...], g1_ref[...]).astype(BF16)
        acc_ref[...] = jnp.zeros(acc_ref.shape, F32)

    a = jnp.dot(h_ref[...], wu_ref[...], preferred_element_type=F32)
    a = jnp.square(jnp.maximum(a, 0.0)).astype(BF16)
    acc_ref[...] += jnp.dot(a, wd_ref[...], preferred_element_type=F32)

    @pl.when(j == pl.num_programs(1) - 1)
    def _():
        o_ref[...] = x_ref[...] + _rms(acc_ref[...], g2_ref[...])


def _mlp(x, g1, wu, wd, g2):
    n = x.shape[0]
    return pl.pallas_call(
        _mlp_kernel,
        grid=(n // TM_MLP, D_FF // TF_MLP),
        in_specs=[
            pl.BlockSpec((TM_MLP, D_MODEL), lambda i, j: (i, 0)),
            pl.BlockSpec((1, D_MODEL), lambda i, j: (0, 0)),
            pl.BlockSpec((D_MODEL, TF_MLP), lambda i, j: (0, j)),
            pl.BlockSpec((TF_MLP, D_MODEL), lambda i, j: (j, 0)),
            pl.BlockSpec((1, D_MODEL), lambda i, j: (0, 0)),
        ],
        out_specs=pl.BlockSpec((TM_MLP, D_MODEL), lambda i, j: (i, 0)),
        out_shape=jax.ShapeDtypeStruct((n, D_MODEL), F32),
        scratch_shapes=[pltpu.VMEM((TM_MLP, D_MODEL), BF16), pltpu.VMEM((TM_MLP, D_MODEL), F32)],
        compiler_params=_params(("arbitrary", "arbitrary")),
        name="mlp",
    )(x, g1, wu, wd, g2)


def _rope_table(n_tokens):
    n_rows = n_tokens // GRID_W
    row = jnp.repeat(jnp.arange(n_rows, dtype=F32), GRID_W)
    col = jnp.tile(jnp.arange(GRID_W, dtype=F32), n_rows)

    def cos_sin(rot_dim):
        axis_dim = rot_dim // 2
        inv_freq = ROPE_THETA ** (-jnp.arange(0, axis_dim, 2, dtype=F32) / axis_dim)
        ang = jnp.concatenate([row[:, None] * inv_freq, col[:, None] * inv_freq], axis=-1)
        return jnp.cos(ang), jnp.sin(ang)

    cm, sm = cos_sin(MLA_ROPE)
    cg, sg = cos_sin(GQA_HEAD_DIM)
    ones = jnp.ones((n_tokens, MLA_NOPE), F32)
    z64 = jnp.zeros((n_tokens, 64), F32)
    z128 = jnp.zeros((n_tokens, MLA_NOPE), F32)
    return jnp.concatenate([
        ones, cm, cm, z64,
        z128, -sm, sm, z64,
        cm, cm, z64,
        -sm, sm, z64,
        cg, cg,
        -sg, sg,
    ], axis=-1)


def _swap_halves(w):
    half = w.shape[-1] // 2
    return jnp.concatenate([w[..., half:], w[..., :half]], axis=-1)


def _ssm_weights(a_re, a_im, log_dt, b_re, b_im, c_re, c_im):
    dt = jnp.exp(log_dt)[:, None]
    mag = jnp.exp(a_re * dt)
    lr = mag * jnp.cos(a_im * dt)
    li = mag * jnp.sin(a_im * dt)
    den = a_re * a_re + a_im * a_im
    fr = ((lr - 1.0) * a_re + li * a_im) / den
    fi = (li * a_re - (lr - 1.0) * a_im) / den
    bb_re = fr[..., None] * b_re - fi[..., None] * b_im
    bb_im = fr[..., None] * b_im + fi[..., None] * b_re

    eye = jnp.eye(SSM_GROUPS, dtype=F32)

    def block_diag_in(bb):
        return jnp.einsum('gph,gk->ghkp', bb, eye).reshape(SSM_WIDTH, SSM_STATES)

    def block_diag_out(c):
        return jnp.einsum('ghp,gk->gpkh', c, eye).reshape(SSM_STATES, SSM_WIDTH)

    win_re, win_im = block_diag_in(bb_re), block_diag_in(bb_im)
    wout_re, wout_im = block_diag_out(c_re), block_diag_out(-c_im)
    hi, hs = SSM_HALF_IN, SSM_HALF_STATES
    wb = jnp.stack([
        jnp.concatenate([win_re[kb * hi:(kb + 1) * hi, kb * hs:(kb + 1) * hs],
                         win_im[kb * hi:(kb + 1) * hi, kb * hs:(kb + 1) * hs]], axis=1)
        for kb in range(2)]).astype(BF16)
    wc = jnp.stack([
        jnp.concatenate([wout_re[kb * hs:(kb + 1) * hs, kb * hi:(kb + 1) * hi],
                         wout_im[kb * hs:(kb + 1) * hs, kb * hi:(kb + 1) * hi]], axis=0)
        for kb in range(2)]).astype(BF16)
    return wb, wc, (a_re * dt).reshape(-1), (a_im * dt).reshape(-1)


def _ssm_tables(log_mag, ang, reverse):
    t = jnp.arange(8, dtype=F32)[:, None]

    def power(k, mask):
        m = jnp.exp(k * log_mag[None, :])
        return jnp.stack([jnp.where(mask, m * jnp.cos(k * ang[None, :]), 0.0),
                          jnp.where(mask, m * jnp.sin(k * ang[None, :]), 0.0)])

    tabs = []
    for d in (1.0, 2.0, 4.0):
        mask = (t + d <= 7.0) if reverse else (t >= d)
        tabs.append(power(jnp.full((8, 1), d, F32), mask))
    k = (8.0 - t) if reverse else (t + 1.0)
    tabs.append(power(k, jnp.ones((8, 1), bool)))
    tab = jnp.stack(tabs)
    hs = SSM_HALF_STATES
    return jnp.stack([tab[..., :hs], tab[..., hs:]])


def _layer_weights(i, pre_mix_norm, w_in, mla_q_norm, mla_w_uq, mla_kv_norm, mla_w_ukv,
                   gqa_q_norm, gqa_k_norm, ssm_a_re, ssm_a_im, ssm_log_dt, ssm_b_re, ssm_b_im,
                   ssm_c_re, ssm_c_im, ssm_d, ssm_w_glu, w_out, post_mix_norm, pre_mlp_norm,
                   w_up, w_down, post_mlp_norm):
    w = w_in[i]
    o1 = MLA_Q_LORA
    o2 = o1 + MLA_KV_LORA
    o3 = o2 + MLA_ROPE
    zpad = jnp.zeros((D_MODEL, 128 - MLA_ROPE), F32)
    w_packed = jnp.concatenate([
        w[:, :o2], w[:, o2:o3], zpad, _swap_halves(w[:, o2:o3]), zpad, w[:, o3:]], axis=1).astype(BF16)

    uq = mla_w_uq[i].reshape(MLA_Q_LORA, MLA_HEADS, MLA_NOPE + MLA_ROPE)
    z_rope = jnp.zeros((MLA_Q_LORA, MLA_HEADS, MLA_QK_PAD - MLA_NOPE - MLA_ROPE), F32)
    z_nope = jnp.zeros((MLA_Q_LORA, MLA_HEADS, MLA_NOPE), F32)
    w_uq_a = jnp.concatenate([uq, z_rope], axis=-1).reshape(MLA_Q_LORA, -1).astype(BF16)
    w_uq_b = jnp.concatenate([z_nope, _swap_halves(uq[..., MLA_NOPE:]), z_rope],
                             axis=-1).reshape(MLA_Q_LORA, -1).astype(BF16)

    wbs, wcs, tabs = [], [], []
    for direction in range(2):
        wb, wc, log_mag, ang = _ssm_weights(
            ssm_a_re[i, direction], ssm_a_im[i, direction], ssm_log_dt[i, direction],
            ssm_b_re[i, direction], ssm_b_im[i, direction], ssm_c_re[i, direction], ssm_c_im[i, direction])
        wbs.append(wb)
        wcs.append(wc)
        tabs.append(_ssm_tables(log_mag, ang, direction == 1))

    row = lambda v: v.reshape(1, -1)
    return dict(
        pre_mix_norm=row(pre_mix_norm[i]), w_packed=w_packed,
        mla_q_norm=row(mla_q_norm[i]), w_uq_a=w_uq_a, w_uq_b=w_uq_b,
        mla_kv_norm=row(mla_kv_norm[i]), w_ukv=mla_w_ukv[i].astype(BF16),
        gqa_q_norm=row(gqa_q_norm[i]), gqa_k_norm=row(gqa_k_norm[i]),
        ssm_wb=wbs, ssm_wc=wcs, ssm_tab=tabs, ssm_d=row(ssm_d[i]), w_glu=ssm_w_glu[i].astype(BF16),
        w_out=w_out[i].astype(BF16), post_mix_norm=row(post_mix_norm[i]),
        pre_mlp_norm=row(pre_mlp_norm[i]), w_up=w_up[i].astype(BF16), w_down=w_down[i].astype(BF16),
        post_mlp_norm=row(post_mlp_norm[i]),
    )


def _trunk(x, layers):
    b, l, _ = x.shape
    n = b * l
    table = _rope_table(l)
    x = x.reshape(n, D_MODEL)
    for lw in layers:
        proj = _in_proj(x, lw["pre_mix_norm"], lw["w_packed"])
        qm, km, vm, qg, kg, vg = _prep(proj, table, l, lw)
        seq = lambda a: a.reshape(b, l, a.shape[-1])
        y_mla = _attention(seq(qm), seq(km), seq(vm), kv_heads=MLA_HEADS, group=1,
                           dk=MLA_QK_PAD, dv=MLA_V)
        y_gqa = _attention(seq(qg), seq(kg), seq(vg), kv_heads=GQA_KV_HEADS, group=GQA_GROUP,
                           dk=GQA_HEAD_DIM, dv=GQA_HEAD_DIM)
        proj3 = seq(proj)
        y_fwd = _ssm_direction(proj3, lw, 0)
        y_ssm = _ssm_direction(proj3, lw, 1, y_fwd)
        x = _out_proj(x, y_mla.reshape(n, -1), y_gqa.reshape(n, -1), y_ssm.reshape(n, -1),
                      lw["w_out"], lw["post_mix_norm"])
        x = _mlp(x, lw["pre_mlp_norm"], lw["w_up"], lw["w_down"], lw["post_mlp_norm"])
    return x.reshape(b, l, D_MODEL)


def kernel(x_prompt, x_sample, pre_mix_norm, w_in, mla_q_norm, mla_w_uq, mla_kv_norm, mla_w_ukv,
           gqa_q_norm, gqa_k_norm, ssm_a_re, ssm_a_im, ssm_log_dt, ssm_b_re, ssm_b_im, ssm_c_re,
           ssm_c_im, ssm_d, ssm_w_glu, w_out, post_mix_norm, pre_mlp_norm, w_up, w_down,
           post_mlp_norm):
    weights = (pre_mix_norm, w_in, mla_q_norm, mla_w_uq, mla_kv_norm, mla_w_ukv, gqa_q_norm,
               gqa_k_norm, ssm_a_re, ssm_a_im, ssm_log_dt, ssm_b_re, ssm_b_im, ssm_c_re, ssm_c_im,
               ssm_d, ssm_w_glu, w_out, post_mix_norm, pre_mlp_norm, w_up, w_down, post_mlp_norm)
    layers = [_layer_weights(i, *weights) for i in range(DEPTH)]
    return _trunk(x_prompt, layers), _trunk(x_sample, layers)
```

```python
import functools
import math

import jax
import jax.numpy as jnp
from jax import lax
from jax.experimental import pallas as pl
from jax.experimental.pallas import tpu as pltpu

F32 = jnp.float32
BF16 = jnp.bfloat16

D_MODEL = 2048
DEPTH = 4
GRID_W = 64
ROPE_THETA = 10000.0
NORM_EPS = 1e-6

MLA_WIDTH = D_MODEL // 4
GQA_WIDTH = D_MODEL // 2
SSM_WIDTH = D_MODEL - MLA_WIDTH - GQA_WIDTH
MLA_NOPE = 128
MLA_ROPE = 64
MLA_V = 128
MLA_HEADS = MLA_WIDTH // MLA_V
MLA_Q_LORA = D_MODEL // 4
MLA_KV_LORA = D_MODEL // 8
MLA_QK_PAD = 256
GQA_HEAD_DIM = 128
GQA_HEADS = GQA_WIDTH // GQA_HEAD_DIM
GQA_KV_HEADS = GQA_HEADS // 4
GQA_GROUP = GQA_HEADS // GQA_KV_HEADS
SSM_GROUP = 16
SSM_GROUPS = SSM_WIDTH // SSM_GROUP
SSM_STATE = 64
SSM_STATES = SSM_GROUPS * SSM_STATE
SSM_HALF_IN = SSM_WIDTH // 2
SSM_HALF_STATES = SSM_STATES // 2
D_FF = 4 * D_MODEL

P_CQ = 0
P_CKV = P_CQ + MLA_Q_LORA
P_KR = P_CKV + MLA_KV_LORA
P_KRS = P_KR + 128
P_GQ = P_KRS + 128
P_GK = P_GQ + GQA_WIDTH
P_GV = P_GK + GQA_KV_HEADS * GQA_HEAD_DIM
P_U = P_GV + GQA_KV_HEADS * GQA_HEAD_DIM
P_COLS = P_U + SSM_WIDTH

T_MC = 0
T_MS = 256
T_KC = 512
T_KS = 640
T_GC = 768
T_GS = 896
T_COLS = 1024

LOG2E = math.log2(math.e)
MLA_QSCALE = (MLA_NOPE + MLA_ROPE) ** -0.5 * LOG2E
GQA_QSCALE = GQA_HEAD_DIM ** -0.5 * LOG2E

VMEM_LIMIT = 56 * 1024 * 1024

TM_PROJ = 1024
TN_PROJ = 1024
TP_PREP = 512
TM_OUT = 512
TM_MLP = 512
TF_MLP = 1024
TK_ATTN = 512
SOFTMAX_ROWS = 32
NQ_ATTN = 1024
TT_SSM = 512


def _rms(x, g):
    ms = jnp.mean(x * x, axis=-1, keepdims=True)
    return x * lax.rsqrt(ms + NORM_EPS) * g


def _params(sem):
    return pltpu.CompilerParams(dimension_semantics=sem, vmem_limit_bytes=VMEM_LIMIT)


def _in_proj_kernel(x_ref, g_ref, w_ref, o_ref, h_ref):
    @pl.when(pl.program_id(1) == 0)
    def _():
        h_ref[...] = _rms(x_ref[...], g_ref[...]).astype(BF16)

    o_ref[...] = jnp.dot(h_ref[...], w_ref[...], preferred_element_type=F32)


def _in_proj(x, g, w):
    n = x.shape[0]
    return pl.pallas_call(
        _in_proj_kernel,
        grid=(n // TM_PROJ, P_COLS // TN_PROJ),
        in_specs=[
            pl.BlockSpec((TM_PROJ, D_MODEL), lambda i, j: (i, 0)),
            pl.BlockSpec((1, D_MODEL), lambda i, j: (0, 0)),
            pl.BlockSpec((D_MODEL, TN_PROJ), lambda i, j: (0, j)),
        ],
        out_specs=pl.BlockSpec((TM_PROJ, TN_PROJ), lambda i, j: (i, j)),
        out_shape=jax.ShapeDtypeStruct((n, P_COLS), F32),
        scratch_shapes=[pltpu.VMEM((TM_PROJ, D_MODEL), BF16)],
        compiler_params=_params(("arbitrary", "arbitrary")),
        name="in_proj",
    )(x, g, w)


def _prep_kernel(p_ref, t_ref, qn_ref, wa_ref, wb_ref, kvn_ref, wkv_ref, gqn_ref, gkn_ref,
                 qm_ref, km_ref, vm_ref, qg_ref, kg_ref, vg_ref):
    nq = _rms(p_ref[:, P_CQ:P_CQ + MLA_Q_LORA], qn_ref[...]).astype(BF16)
    a = jnp.dot(nq, wa_ref[...], preferred_element_type=F32)
    b = jnp.dot(nq, wb_ref[...], preferred_element_type=F32)
    cos_m = t_ref[:, T_MC:T_MC + MLA_QK_PAD]
    sin_m = t_ref[:, T_MS:T_MS + MLA_QK_PAD]
    for h in range(MLA_HEADS):
        sl = slice(h * MLA_QK_PAD, (h + 1) * MLA_QK_PAD)
        qm_ref[:, sl] = ((a[:, sl] * cos_m + b[:, sl] * sin_m) * MLA_QSCALE).astype(BF16)

    nkv = _rms(p_ref[:, P_CKV:P_CKV + MLA_KV_LORA], kvn_ref[...]).astype(BF16)
    kv = jnp.dot(nkv, wkv_ref[...], preferred_element_type=F32)
    k_rot = (p_ref[:, P_KR:P_KR + 128] * t_ref[:, T_KC:T_KC + 128]
             + p_ref[:, P_KRS:P_KRS + 128] * t_ref[:, T_KS:T_KS + 128]).astype(BF16)
    for h in range(MLA_HEADS):
        base = h * (MLA_NOPE + MLA_V)
        km_ref[:, h * MLA_QK_PAD:h * MLA_QK_PAD + MLA_NOPE] = kv[:, base:base + MLA_NOPE].astype(BF16)
        km_ref[:, h * MLA_QK_PAD + MLA_NOPE:(h + 1) * MLA_QK_PAD] = k_rot
        vm_ref[:, h * MLA_V:(h + 1) * MLA_V] = kv[:, base + MLA_NOPE:base + MLA_NOPE + MLA_V].astype(BF16)

    cos_g = t_ref[:, T_GC:T_GC + 128]
    sin_g = t_ref[:, T_GS:T_GS + 128]

    def rope_g(x, g):
        y = _rms(x, g)
        return y * cos_g + pltpu.roll(y, GQA_HEAD_DIM // 2, 1) * sin_g

    for h in range(GQA_HEADS):
        sl = slice(h * GQA_HEAD_DIM, (h + 1) * GQA_HEAD_DIM)
        x = p_ref[:, P_GQ + h * GQA_HEAD_DIM:P_GQ + (h + 1) * GQA_HEAD_DIM]
        qg_ref[:, sl] = (rope_g(x, gqn_ref[...]) * GQA_QSCALE).astype(BF16)
    for h in range(GQA_KV_HEADS):
        sl = slice(h * GQA_HEAD_DIM, (h + 1) * GQA_HEAD_DIM)
        x = p_ref[:, P_GK + h * GQA_HEAD_DIM:P_GK + (h + 1) * GQA_HEAD_DIM]
        kg_ref[:, sl] = rope_g(x, gkn_ref[...]).astype(BF16)
    vg_ref[...] = p_ref[:, P_GV:P_GV + GQA_KV_HEADS * GQA_HEAD_DIM].astype(BF16)


def _prep(proj, table, seq_len, lw):
    n = proj.shape[0]
    t_blocks = seq_len // TP_PREP
    full = lambda r, c: pl.BlockSpec((r, c), lambda i: (0, 0))
    row = lambda c: pl.BlockSpec((TP_PREP, c), lambda i: (i, 0))
    kvw = GQA_KV_HEADS * GQA_HEAD_DIM
    return pl.pallas_call(
        _prep_kernel,
        grid=(n // TP_PREP,),
        in_specs=[
            row(P_COLS),
            pl.BlockSpec((TP_PREP, T_COLS), lambda i: (i % t_blocks, 0)),
            full(1, MLA_Q_LORA), full(MLA_Q_LORA, MLA_HEADS * MLA_QK_PAD),
            full(MLA_Q_LORA, MLA_HEADS * MLA_QK_PAD),
            full(1, MLA_KV_LORA), full(MLA_KV_LORA, MLA_HEADS * (MLA_NOPE + MLA_V)),
            full(1, GQA_HEAD_DIM), full(1, GQA_HEAD_DIM),
        ],
        out_specs=[row(MLA_HEADS * MLA_QK_PAD), row(MLA_HEADS * MLA_QK_PAD), row(MLA_WIDTH),
                   row(GQA_WIDTH), row(kvw), row(kvw)],
        out_shape=[
            jax.ShapeDtypeStruct((n, MLA_HEADS * MLA_QK_PAD), BF16),
            jax.ShapeDtypeStruct((n, MLA_HEADS * MLA_QK_PAD), BF16),
            jax.ShapeDtypeStruct((n, MLA_WIDTH), BF16),
            jax.ShapeDtypeStruct((n, GQA_WIDTH), BF16),
            jax.ShapeDtypeStruct((n, kvw), BF16),
            jax.ShapeDtypeStruct((n, kvw), BF16),
        ],
        compiler_params=_params(("arbitrary",)),
        name="prep",
    )(proj, table, lw["mla_q_norm"], lw["w_uq_a"], lw["w_uq_b"], lw["mla_kv_norm"], lw["w_ukv"],
      lw["gqa_q_norm"], lw["gqa_k_norm"])


def _attn_kernel(q_ref, k_ref, vt_ref, o_ref, m_ref, acc_ref, s_ref, cm_ref, p_ref, al_ref,
                 *, group, tq, tk, nk, dk, dv):
    if group == 1:
        q = q_ref[0]
    else:
        q = jnp.concatenate([q_ref[0, :, g * dk:(g + 1) * dk] for g in range(group)], axis=0)

    def scores(ki, slot):
        k = k_ref[0, pl.ds(pl.multiple_of(ki * tk, tk), tk), :]
        s = lax.dot_general(k, q, (((1,), (1,)), ((), ())), preferred_element_type=F32)
        s_ref[slot] = s
        cm_ref[slot] = jnp.max(s, axis=0, keepdims=True)

    def softmax(slot):
        m_prev = m_ref[...]
        m_new = jnp.maximum(m_prev, cm_ref[slot])
        for r in range(0, tk, SOFTMAX_ROWS):
            p_ref[slot, r:r + SOFTMAX_ROWS, :] = jnp.exp2(s_ref[slot, r:r + SOFTMAX_ROWS, :] - m_new).astype(BF16)
        al_ref[slot] = jnp.exp2(m_prev - m_new)
        m_ref[...] = m_new

    def weighted_values(ki, slot):
        pv = jnp.dot(vt_ref[0, 0, ki], p_ref[slot], preferred_element_type=F32)
        acc_ref[...] = al_ref[slot] * acc_ref[...] + pv

    m_ref[...] = jnp.full(m_ref.shape, -jnp.inf, F32)
    acc_ref[...] = jnp.zeros(acc_ref.shape, F32)
    for slot in (2, 3):
        p_ref[slot] = jnp.zeros(p_ref.shape[1:], BF16)
        al_ref[slot] = jnp.ones(al_ref.shape[1:], F32)
    scores(0, 0)
    scores(1, 1)

    def step(i, last):
        scores(i + 2, 2)
        scores(i + 3, 3)
        softmax(0)
        softmax(1)
        weighted_values(jnp.maximum(i - 2, 0), 2)
        weighted_values(jnp.maximum(i - 1, 0), 3)
        if not last:
            scores(i + 4, 0)
            scores(i + 5, 1)
        softmax(2)
        softmax(3)
        weighted_values(i, 0)
        weighted_values(i + 1, 1)

    def body(j, carry):
        step(4 * j, False)
        return carry

    lax.fori_loop(0, nk // 4 - 1, body, 0)
    step(nk - 4, True)
    weighted_values(nk - 2, 2)
    weighted_values(nk - 1, 3)
    o = (acc_ref[:dv, :] / acc_ref[dv:dv + 1, :]).T
    for g in range(group):
        o_ref[0, :, g * dv:(g + 1) * dv] = o[g * tq:(g + 1) * tq].astype(BF16)


def _attention(q, k, v, *, kv_heads, group, dk, dv):
    b, l, _ = q.shape
    tq = NQ_ATTN // group
    tk = TK_ATTN
    nk = l // tk
    vt = v.reshape(b, nk, tk, kv_heads, dv).transpose(0, 3, 1, 4, 2)
    vt = jnp.concatenate([vt, jnp.ones((b, kv_heads, nk, 8, tk), BF16)], axis=3)
    kern = functools.partial(_attn_kernel, group=group, tq=tq, tk=tk, nk=nk, dk=dk, dv=dv)
    return pl.pallas_call(
        kern,
        grid=(b, kv_heads, l // tq),
        in_specs=[
            pl.BlockSpec((1, tq, group * dk), lambda bi, gi, qi: (bi, qi, gi)),
            pl.BlockSpec((1, l, dk), lambda bi, gi, qi: (bi, 0, gi)),
            pl.BlockSpec((1, 1, nk, dv + 8, tk), lambda bi, gi, qi: (bi, gi, 0, 0, 0)),
        ],
        out_specs=pl.BlockSpec((1, tq, group * dv), lambda bi, gi, qi: (bi, qi, gi)),
        out_shape=jax.ShapeDtypeStruct((b, l, kv_heads * group * dv), BF16),
        scratch_shapes=[
            pltpu.VMEM((1, NQ_ATTN), F32),
            pltpu.VMEM((dv + 8, NQ_ATTN), F32),
            pltpu.VMEM((4, tk, NQ_ATTN), F32),
            pltpu.VMEM((4, 1, NQ_ATTN), F32),
            pltpu.VMEM((4, tk, NQ_ATTN), BF16),
            pltpu.VMEM((4, 1, NQ_ATTN), F32),
        ],
        compiler_params=_params(("arbitrary", "arbitrary", "arbitrary")),
        name="attention",
    )(q, k, vt)


def _ssm_kernel(*refs, tt, reverse):
    if reverse:
        (u_ref, wb_ref, wc_ref, lam_ref, tab_ref, yf_ref, d_ref, wglu_ref, unperm_ref, o_ref, s_ref,
         carry_ref) = refs
    else:
        u_ref, wb_ref, wc_ref, lam_ref, tab_ref, o_ref, s_ref, carry_ref = refs
    sub = tt // 8
    hs = SSM_HALF_STATES
    order = range(sub - 1, -1, -1) if reverse else range(sub)
    edge = 7 if reverse else 0

    @pl.when(pl.program_id(1) == 0)
    def _():
        carry_ref[...] = jnp.zeros(carry_ref.shape, F32)

    u = u_ref[0]
    ub = u.astype(BF16)
    for kb in range(2):
        s_ref[kb] = jnp.dot(ub[:, kb * SSM_HALF_IN:(kb + 1) * SSM_HALF_IN], wb_ref[kb],
                            preferred_element_type=F32)

    def get(kb, r, lo):
        return s_ref[kb, 8 * r:8 * r + 8, lo:lo + hs]

    def put(kb, r, lo, val):
        s_ref[kb, 8 * r:8 * r + 8, lo:lo + hs] = val

    ys = []
    for kb in range(2):
        lam_r = lam_ref[kb, 0]
        lam_i = lam_ref[kb, 1]
        xr = jnp.zeros((8, hs), F32)
        xi = jnp.zeros((8, hs), F32)
        for r in order:
            xr, xi = (lam_r * xr - lam_i * xi + get(kb, r, 0),
                      lam_r * xi + lam_i * xr + get(kb, r, hs))
            put(kb, r, 0, xr)
            put(kb, r, hs, xi)
        for j, d in enumerate((1, 2, 4)):
            sh = (8 - d) if reverse else d
            sr = pltpu.roll(xr, sh, 0)
            si = pltpu.roll(xi, sh, 0)
            lr = tab_ref[kb, j, 0]
            li = tab_ref[kb, j, 1]
            xr, xi = xr + lr * sr - li * si, xi + lr * si + li * sr
        cr = carry_ref[kb, 0]
        ci = carry_ref[kb, 1]
        lr = tab_ref[kb, 3, 0]
        li = tab_ref[kb, 3, 1]
        xr, xi = xr + lr * cr - li * ci, xi + lr * ci + li * cr
        last = 0 if reverse else 7
        carry_ref[kb, 0] = xr[last:last + 1]
        carry_ref[kb, 1] = xi[last:last + 1]
        sublane = lax.broadcasted_iota(jnp.int32, (8, hs), 0)
        shift = 7 if reverse else 1
        hr = jnp.where(sublane == edge, cr, pltpu.roll(xr, shift, 0))
        hi = jnp.where(sublane == edge, ci, pltpu.roll(xi, shift, 0))
        for r in order:
            hr, hi = lam_r * hr - lam_i * hi, lam_r * hi + lam_i * hr
            put(kb, r, 0, get(kb, r, 0) + hr)
            put(kb, r, hs, get(kb, r, hs) + hi)
        ys.append(jnp.dot(s_ref[kb].astype(BF16), wc_ref[kb], preferred_element_type=F32))
    y = jnp.concatenate(ys, axis=1)
    if reverse:
        y = jax.nn.gelu(d_ref[...] * u + yf_ref[0] + y)
        gate = jnp.dot(y.astype(BF16), wglu_ref[...], preferred_element_type=F32)
        out = (y * jax.nn.sigmoid(gate)).astype(BF16)
        o_ref[0] = jnp.dot(unperm_ref[...], out, preferred_element_type=F32).astype(BF16)
    else:
        o_ref[0] = y


def _interleave_rows(u, tt):
    b, l, c = u.shape
    return u.reshape(b, l // tt, 8, tt // 8, c).transpose(0, 1, 3, 2, 4).reshape(b, l, c)


def _uninterleave_matrix(tt):
    new_row = jnp.arange(tt)
    old_row = (new_row % 8) * (tt // 8) + new_row // 8
    return (old_row[None, :] == jnp.arange(tt)[:, None]).astype(BF16)


def _ssm_direction(u_il, lw, direction, y_fwd=None):
    b, l, _ = u_il.shape
    tt = TT_SSM
    nt = l // tt
    reverse = direction == 1
    tmap = (lambda bi, ti: (bi, nt - 1 - ti, 0)) if reverse else (lambda bi, ti: (bi, ti, 0))
    const = lambda shape: pl.BlockSpec(shape, lambda bi, ti: (0,) * len(shape))
    in_specs = [
        pl.BlockSpec((1, tt, SSM_WIDTH), tmap),
        const((2, SSM_HALF_IN, SSM_STATES)),
        const((2, SSM_STATES, SSM_HALF_IN)),
        const((2, 2, 8, SSM_HALF_STATES)),
        const((2, 4, 2, 8, SSM_HALF_STATES)),
    ]
    args = [u_il, lw["ssm_wb"][direction], lw["ssm_wc"][direction], lw["ssm_lam"][direction],
            lw["ssm_tab"][direction]]
    if reverse:
        in_specs += [pl.BlockSpec((1, tt, SSM_WIDTH), tmap), const((1, SSM_WIDTH)),
                     const((SSM_WIDTH, SSM_WIDTH)), const((tt, tt))]
        args += [y_fwd, lw["ssm_d"], lw["w_glu"], _uninterleave_matrix(tt)]
    return pl.pallas_call(
        functools.partial(_ssm_kernel, tt=tt, reverse=reverse),
        grid=(b, nt),
        in_specs=in_specs,
        out_specs=pl.BlockSpec((1, tt, SSM_WIDTH), tmap),
        out_shape=jax.ShapeDtypeStruct((b, l, SSM_WIDTH), BF16 if reverse else F32),
        scratch_shapes=[
            pltpu.VMEM((2, tt, SSM_STATES), F32),
            pltpu.VMEM((2, 2, 1, SSM_HALF_STATES), F32),
        ],
        compiler_params=_params(("arbitrary", "arbitrary")),
        name="ssm_bwd" if reverse else "ssm_fwd",
    )(*args)


def _out_proj_kernel(x_ref, ym_ref, yg_ref, ys_ref, w_ref, g_ref, o_ref):
    o1 = MLA_WIDTH
    o2 = o1 + GQA_WIDTH
    mix = jnp.dot(ym_ref[...], w_ref[:o1, :], preferred_element_type=F32)
    mix += jnp.dot(yg_ref[...], w_ref[o1:o2, :], preferred_element_type=F32)
    mix += jnp.dot(ys_ref[...], w_ref[o2:, :], preferred_element_type=F32)
    o_ref[...] = x_ref[...] + _rms(mix, g_ref[...])


def _out_proj(x, ym, yg, ys, w, g):
    n = x.shape[0]
    row = lambda c: pl.BlockSpec((TM_OUT, c), lambda i: (i, 0))
    return pl.pallas_call(
        _out_proj_kernel,
        grid=(n // TM_OUT,),
        in_specs=[row(D_MODEL), row(MLA_WIDTH), row(GQA_WIDTH), row(SSM_WIDTH),
                  pl.BlockSpec((D_MODEL, D_MODEL), lambda i: (0, 0)),
                  pl.BlockSpec((1, D_MODEL), lambda i: (0, 0))],
        out_specs=row(D_MODEL),
        out_shape=jax.ShapeDtypeStruct((n, D_MODEL), F32),
        compiler_params=_params(("arbitrary",)),
        name="out_proj",
    )(x, ym, yg, ys, w, g)


def _mlp_kernel(x_ref, g1_ref, wu_ref, wd_ref, g2_ref, o_ref, h_ref, acc_ref):
    j = pl.program_id(1)

    @pl.when(j == 0)
    def _():
        h_ref[...] = _rms(x_ref[...], g1_ref[...]).astype(BF16)
        acc_ref[...] = jnp.zeros(acc_ref.shape, F32)

    a = jnp.dot(h_ref[...], wu_ref[...], preferred_element_type=F32)
    a = jnp.square(jnp.maximum(a, 0.0)).astype(BF16)
    acc_ref[...] += jnp.dot(a, wd_ref[...], preferred_element_type=F32)

    @pl.when(j == pl.num_programs(1) - 1)
    def _():
        o_ref[...] = x_ref[...] + _rms(acc_ref[...], g2_ref[...])


def _mlp(x, g1, wu, wd, g2):
    n = x.shape[0]
    return pl.pallas_call(
        _mlp_kernel,
        grid=(n // TM_MLP, D_FF // TF_MLP),
        in_specs=[
            pl.BlockSpec((TM_MLP, D_MODEL), lambda i, j: (i, 0)),
            pl.BlockSpec((1, D_MODEL), lambda i, j: (0, 0)),
            pl.BlockSpec((D_MODEL, TF_MLP), lambda i, j: (0, j)),
            pl.BlockSpec((TF_MLP, D_MODEL), lambda i, j: (j, 0)),
            pl.BlockSpec((1, D_MODEL), lambda i, j: (0, 0)),
        ],
        out_specs=pl.BlockSpec((TM_MLP, D_MODEL), lambda i, j: (i, 0)),
        out_shape=jax.ShapeDtypeStruct((n, D_MODEL), F32),
        scratch_shapes=[pltpu.VMEM((TM_MLP, D_MODEL), BF16), pltpu.VMEM((TM_MLP, D_MODEL), F32)],
        compiler_params=_params(("arbitrary", "arbitrary")),
        name="mlp",
    )(x, g1, wu, wd, g2)


def _rope_table(n_tokens):
    n_rows = n_tokens // GRID_W
    row = jnp.repeat(jnp.arange(n_rows, dtype=F32), GRID_W)
    col = jnp.tile(jnp.arange(GRID_W, dtype=F32), n_rows)

    def cos_sin(rot_dim):
        axis_dim = rot_dim // 2
        inv_freq = ROPE_THETA ** (-jnp.arange(0, axis_dim, 2, dtype=F32) / axis_dim)
        ang = jnp.concatenate([row[:, None] * inv_freq, col[:, None] * inv_freq], axis=-1)
        return jnp.cos(ang), jnp.sin(ang)

    cm, sm = cos_sin(MLA_ROPE)
    cg, sg = cos_sin(GQA_HEAD_DIM)
    ones = jnp.ones((n_tokens, MLA_NOPE), F32)
    z64 = jnp.zeros((n_tokens, 64), F32)
    z128 = jnp.zeros((n_tokens, MLA_NOPE), F32)
    return jnp.concatenate([
        ones, cm, cm, z64,
        z128, -sm, sm, z64,
        cm, cm, z64,
        -sm, sm, z64,
        cg, cg,
        -sg, sg,
    ], axis=-1)


def _swap_halves(w):
    half = w.shape[-1] // 2
    return jnp.concatenate([w[..., half:], w[..., :half]], axis=-1)


def _ssm_weights(a_re, a_im, log_dt, b_re, b_im, c_re, c_im):
    dt = jnp.exp(log_dt)[:, None]
    mag = jnp.exp(a_re * dt)
    lr = mag * jnp.cos(a_im * dt)
    li = mag * jnp.sin(a_im * dt)
    den = a_re * a_re + a_im * a_im
    fr = ((lr - 1.0) * a_re + li * a_im) / den
    fi = (li * a_re - (lr - 1.0) * a_im) / den
    bb_re = fr[..., None] * b_re - fi[..., None] * b_im
    bb_im = fr[..., None] * b_im + fi[..., None] * b_re

    eye = jnp.eye(SSM_GROUPS, dtype=F32)

    def block_diag_in(bb):
        return jnp.einsum('gph,gk->ghkp', bb, eye).reshape(SSM_WIDTH, SSM_STATES)

    def block_diag_out(c):
        return jnp.einsum('ghp,gk->gpkh', c, eye).reshape(SSM_STATES, SSM_WIDTH)

    win_re, win_im = block_diag_in(bb_re), block_diag_in(bb_im)
    wout_re, wout_im = block_diag_out(c_re), block_diag_out(-c_im)
    hi, hs = SSM_HALF_IN, SSM_HALF_STATES
    wb = jnp.stack([
        jnp.concatenate([win_re[kb * hi:(kb + 1) * hi, kb * hs:(kb + 1) * hs],
                         win_im[kb * hi:(kb + 1) * hi, kb * hs:(kb + 1) * hs]], axis=1)
        for kb in range(2)]).astype(BF16)
    wc = jnp.stack([
        jnp.concatenate([wout_re[kb * hs:(kb + 1) * hs, kb * hi:(kb + 1) * hi],
                         wout_im[kb * hs:(kb + 1) * hs, kb * hi:(kb + 1) * hi]], axis=0)
        for kb in range(2)]).astype(BF16)
    return wb, wc, (a_re * dt).reshape(-1), (a_im * dt).reshape(-1)


def _ssm_tables(log_mag, ang, reverse):
    t = jnp.arange(8, dtype=F32)[:, None]

    def power(k, mask):
        m = jnp.exp(k * log_mag[None, :])
        return jnp.stack([jnp.where(mask, m * jnp.cos(k * ang[None, :]), 0.0),
                          jnp.where(mask, m * jnp.sin(k * ang[None, :]), 0.0)])

    tabs = []
    for d in (1.0, 2.0, 4.0):
        mask = (t + d <= 7.0) if reverse else (t >= d)
        tabs.append(power(jnp.full((8, 1), d, F32), mask))
    k = (8.0 - t) if reverse else (t + 1.0)
    tabs.append(power(k, jnp.ones((8, 1), bool)))
    tab = jnp.stack(tabs)
    hs = SSM_HALF_STATES
    return jnp.stack([tab[..., :hs], tab[..., hs:]])


def _layer_weights(i, pre_mix_norm, w_in, mla_q_norm, mla_w_uq, mla_kv_norm, mla_w_ukv,
                   gqa_q_norm, gqa_k_norm, ssm_a_re, ssm_a_im, ssm_log_dt, ssm_b_re, ssm_b_im,
                   ssm_c_re, ssm_c_im, ssm_d, ssm_w_glu, w_out, post_mix_norm, pre_mlp_norm,
                   w_up, w_down, post_mlp_norm):
    w = w_in[i]
    o1 = MLA_Q_LORA
    o2 = o1 + MLA_KV_LORA
    o3 = o2 + MLA_ROPE
    zpad = jnp.zeros((D_MODEL, 128 - MLA_ROPE), F32)
    w_packed = jnp.concatenate([
        w[:, :o2], w[:, o2:o3], zpad, _swap_halves(w[:, o2:o3]), zpad, w[:, o3:]], axis=1).astype(BF16)

    uq = mla_w_uq[i].reshape(MLA_Q_LORA, MLA_HEADS, MLA_NOPE + MLA_ROPE)
    z_rope = jnp.zeros((MLA_Q_LORA, MLA_HEADS, MLA_QK_PAD - MLA_NOPE - MLA_ROPE), F32)
    z_nope = jnp.zeros((MLA_Q_LORA, MLA_HEADS, MLA_NOPE), F32)
    w_uq_a = jnp.concatenate([uq, z_rope], axis=-1).reshape(MLA_Q_LORA, -1).astype(BF16)
    w_uq_b = jnp.concatenate([z_nope, _swap_halves(uq[..., MLA_NOPE:]), z_rope],
                             axis=-1).reshape(MLA_Q_LORA, -1).astype(BF16)

    wbs, wcs, lams, tabs = [], [], [], []
    sub = TT_SSM // 8
    hs = SSM_HALF_STATES
    for direction in range(2):
        wb, wc, log_mag, ang = _ssm_weights(
            ssm_a_re[i, direction], ssm_a_im[i, direction], ssm_log_dt[i, direction],
            ssm_b_re[i, direction], ssm_b_im[i, direction], ssm_c_re[i, direction], ssm_c_im[i, direction])
        wbs.append(wb)
        wcs.append(wc)
        lam = jnp.stack([jnp.exp(log_mag) * jnp.cos(ang), jnp.exp(log_mag) * jnp.sin(ang)])
        lam = jnp.broadcast_to(lam[:, None, :], (2, 8, SSM_STATES))
        lams.append(jnp.stack([lam[..., :hs], lam[..., hs:]]))
        tabs.append(_ssm_tables(log_mag * sub, ang * sub, direction == 1))

    row = lambda v: v.reshape(1, -1)
    return dict(
        pre_mix_norm=row(pre_mix_norm[i]), w_packed=w_packed,
        mla_q_norm=row(mla_q_norm[i]), w_uq_a=w_uq_a, w_uq_b=w_uq_b,
        mla_kv_norm=row(mla_kv_norm[i]), w_ukv=mla_w_ukv[i].astype(BF16),
        gqa_q_norm=row(gqa_q_norm[i]), gqa_k_norm=row(gqa_k_norm[i]),
        ssm_wb=wbs, ssm_wc=wcs, ssm_lam=lams, ssm_tab=tabs, ssm_d=row(ssm_d[i]), w_glu=ssm_w_glu[i].astype(BF16),
        w_out=w_out[i].astype(BF16), post_mix_norm=row(post_mix_norm[i]),
        pre_mlp_norm=row(pre_mlp_norm[i]), w_up=w_up[i].astype(BF16), w_down=w_down[i].astype(BF16),
        post_mlp_norm=row(post_mlp_norm[i]),
    )


def _trunk(x, layers):
    b, l, _ = x.shape
    n = b * l
    table = _rope_table(l)
    x = x.reshape(n, D_MODEL)
    for lw in layers:
        proj = _in_proj(x, lw["pre_mix_norm"], lw["w_packed"])
        qm, km, vm, qg, kg, vg = _prep(proj, table, l, lw)
        seq = lambda a: a.reshape(b, l, a.shape[-1])
        y_mla = _attention(seq(qm), seq(km), seq(vm), kv_heads=MLA_HEADS, group=1,
                           dk=MLA_QK_PAD, dv=MLA_V)
        y_gqa = _attention(seq(qg), seq(kg), seq(vg), kv_heads=GQA_KV_HEADS, group=GQA_GROUP,
                           dk=GQA_HEAD_DIM, dv=GQA_HEAD_DIM)
        u_il = _interleave_rows(seq(proj[:, P_U:]), TT_SSM)
        y_fwd = _ssm_direction(u_il, lw, 0)
        y_ssm = _ssm_direction(u_il, lw, 1, y_fwd)
        x = _out_proj(x, y_mla.reshape(n, -1), y_gqa.reshape(n, -1), y_ssm.reshape(n, -1),
                      lw["w_out"], lw["post_mix_norm"])
        x = _mlp(x, lw["pre_mlp_norm"], lw["w_up"], lw["w_down"], lw["post_mlp_norm"])
    return x.reshape(b, l, D_MODEL)


def kernel(x_prompt, x_sample, pre_mix_norm, w_in, mla_q_norm, mla_w_uq, mla_kv_norm, mla_w_ukv,
           gqa_q_norm, gqa_k_norm, ssm_a_re, ssm_a_im, ssm_log_dt, ssm_b_re, ssm_b_im, ssm_c_re,
           ssm_c_im, ssm_d, ssm_w_glu, w_out, post_mix_norm, pre_mlp_norm, w_up, w_down,
           post_mlp_norm):
    weights = (pre_mix_norm, w_in, mla_q_norm, mla_w_uq, mla_kv_norm, mla_w_ukv, gqa_q_norm,
               gqa_k_norm, ssm_a_re, ssm_a_im, ssm_log_dt, ssm_b_re, ssm_b_im, ssm_c_re, ssm_c_im,
               ssm_d, ssm_w_glu, w_out, post_mix_norm, pre_mlp_norm, w_up, w_down, post_mlp_norm)
    layers = [_layer_weights(i, *weights) for i in range(DEPTH)]
    return _trunk(x_prompt, layers), _trunk(x_sample, layers)
```
